```python
import math
import jax
import jax.numpy as jnp
from jax import lax
import numpy as np

D_MODEL = 1024
BATCH = 4
SEQ = 4096
DEPTH = 2
DEC_BATCH = 32
DEC_SEQ = 8
PAST_LEN = 8192
PAGE_SIZE = 128

HG_HEADS = 4
HG_DK = 128
HG_DV = 128
DA_HEADS = 4
DA_DQK = 64
DA_DV = 2 * DA_DQK
ROPE_THETA = 10000.0
GD_HEADS = 4
GD_DK = 128
GD_DV = 128
GD_CONV = 4
GD_CONV_CH = GD_HEADS * (2 * GD_DK + GD_DV)
D_FF = 2816
FFN_CONV = 3
CHUNK = 64
Q_BLOCK = 128
EPS = 1e-6
NEG_INF = -1e30

IN_SIZES = (HG_HEADS * HG_DK, HG_HEADS * HG_DK, HG_HEADS * HG_DV, HG_HEADS * HG_DV,
            DA_HEADS * 2 * DA_DQK, DA_HEADS * 2 * DA_DQK, DA_HEADS * DA_DV,
            GD_HEADS * GD_DK, GD_HEADS * GD_DK, GD_HEADS * GD_DV, GD_HEADS * GD_DV,
            GD_HEADS, GD_HEADS,
            D_MODEL, D_MODEL, D_MODEL)
N_IN = sum(IN_SIZES)

kernel_name = 'hybrid_hgrn2_diffattn_gdn_convffn_step'


def _split_points(sizes):
    pts, acc = [], 0
    for s in sizes[:-1]:
        acc += s
        pts.append(acc)
    return pts


def rmsnorm(x, g):
    xf = x.astype(jnp.float32)
    y = xf * lax.rsqrt(jnp.mean(xf * xf, axis=-1, keepdims=True) + EPS)
    return (y * g.astype(jnp.float32)).astype(x.dtype)


def l2norm(x):
    xf = x.astype(jnp.float32)
    return xf * lax.rsqrt(jnp.sum(xf * xf, axis=-1, keepdims=True) + EPS)


def rope(x, pos):
    d = x.shape[-1]
    half = d // 2
    inv = 1.0 / (ROPE_THETA ** (jnp.arange(0, d, 2, dtype=jnp.float32) / d))
    ang = pos.astype(jnp.float32)[:, None] * inv[None, :]
    cos = jnp.cos(ang)[None, :, None, None, :]
    sin = jnp.sin(ang)[None, :, None, None, :]
    xf = x.astype(jnp.float32)
    x1, x2 = xf[..., :half], xf[..., half:]
    return jnp.concatenate([x1 * cos - x2 * sin, x2 * cos + x1 * sin], axis=-1).astype(x.dtype)


def causal_dwconv(x, buf, w):
    K = w.shape[0]
    L = x.shape[1]
    xp = jnp.concatenate([buf.astype(x.dtype), x], axis=1)
    out = sum(xp[:, j:j + L] * w[j].astype(x.dtype) for j in range(K))
    return out, xp[:, L:]


def _to_chunks(a, c, n):
    b, L = a.shape[:2]
    a = jnp.pad(a, [(0, 0), (0, n * c - L)] + [(0, 0)] * (a.ndim - 2))
    a = a.reshape((b, n, c) + a.shape[2:])
    return jnp.swapaxes(jnp.moveaxis(a, 1, 0), 2, 3)


def _from_chunks(o, L):
    n, b, h, c, dv = o.shape
    return o.transpose(1, 0, 3, 2, 4).reshape(b, n * c, h, dv)[:, :L]


def gla_chunked(q, k, v, logf, s0):
    f32 = jnp.float32
    L = q.shape[1]
    c = min(CHUNK, L)
    n = -(-L // c)
    qs, ks, vs, gs = [_to_chunks(a.astype(f32), c, n) for a in (q, k, v, logf)]
    incl = jnp.tril(jnp.ones((c, c), dtype=bool))

    def step(S, xs):
        qc, kc, vc, gc = xs
        G = jnp.cumsum(gc, axis=2)
        diff = jnp.where(incl[:, :, None], G[:, :, :, None, :] - G[:, :, None, :, :], -jnp.inf)
        A = jnp.einsum('bhtd,bhsd,bhtsd->bhts', qc, kc, jnp.exp(diff))
        o = jnp.einsum('bhtd,bhdv->bhtv', qc * jnp.exp(G), S) + jnp.einsum('bhts,bhsv->bhtv', A, vc)
        GL = G[:, :, -1:, :]
        S = jnp.exp(GL[:, :, 0, :])[..., None] * S + jnp.einsum('bhsd,bhsv->bhdv', kc * jnp.exp(GL - G), vc)
        return S, o

    S, o = lax.scan(step, s0.astype(f32), (qs, ks, vs, gs))
    return _from_chunks(o, L), S


def gdn_chunked(q, k, v, beta, loga, s0):
    f32 = jnp.float32
    L = q.shape[1]
    c = min(CHUNK, L)
    n = -(-L // c)
    qs, ks, vs, bs, gs = [_to_chunks(a.astype(f32), c, n) for a in (q, k, v, beta, loga)]
    incl = jnp.tril(jnp.ones((c, c), dtype=bool))
    strict = jnp.tril(jnp.ones((c, c), dtype=bool), -1)

    def step(S, xs):
        qc, kc, vc, bc, gc = xs
        G = jnp.cumsum(gc, axis=-1)
        D = jnp.exp(jnp.where(incl, G[..., :, None] - G[..., None, :], -jnp.inf))
        KK = jnp.einsum('bhtd,bhsd->bhts', kc, kc)
        A = jnp.where(strict, bc[..., :, None] * D * KK, 0.0)
        eg = jnp.exp(G)[..., None]
        rhs = bc[..., None] * (vc - eg * jnp.einsum('bhtd,bhdv->bhtv', kc, S))
        U = lax.linalg.triangular_solve(A, rhs, left_side=True, lower=True, unit_diagonal=True)
        QK = jnp.einsum('bhtd,bhsd->bhts', qc, kc) * D
        o = eg * jnp.einsum('bhtd,bhdv->bhtv', qc, S) + jnp.einsum('bhts,bhsv->bhtv', QK, U)
        GL = G[..., -1]
        S = jnp.exp(GL)[..., None, None] * S + jnp.einsum(
            'bhsd,bhsv->bhdv', kc * jnp.exp(GL[..., None] - G)[..., None], U)
        return S, o

    S, o = lax.scan(step, s0.astype(f32), (qs, ks, vs, bs, gs))
    return _from_chunks(o, L), S


def diff_attention(q, k, v, lam, q_pos, k_pos):
    b, lq, h, _, d = q.shape
    blk = Q_BLOCK if lq % Q_BLOCK == 0 else lq
    nb = lq // blk
    qb = jnp.moveaxis(q.reshape(b, nb, blk, h, 2, d), 1, 0)
    pb = q_pos.reshape(nb, blk)
    scale = d ** -0.5

    def one_block(args):
        qi, pi = args
        s = jnp.einsum('bqhcd,bkhcd->bhcqk', qi, k, preferred_element_type=jnp.float32) * scale
        s = jnp.where(k_pos[None, :] <= pi[:, None], s, NEG_INF)
        p = jax.nn.softmax(s, axis=-1)
        wts = p[:, :, 0] - lam * p[:, :, 1]
        return jnp.einsum('bhqk,bkhd->bqhd', wts.astype(v.dtype), v)

    o = lax.map(one_block, (qb, pb))
    return jnp.moveaxis(o, 0, 1).reshape(b, lq, h, v.shape[-1])


def decoder_layer(x, pos0, layer_idx, lb, hg_s0, past_k, past_v, gd_s0, gd_buf, ffn_buf, w):
    f32 = jnp.float32
    b, L, _ = x.shape
    pos = pos0 + jnp.arange(L)
    h = rmsnorm(x, w['ln_mix'])
    (hq, hf, hi, hog, dq, dk, dv, gq, gk, gv, gz, gb, ga,
     gate_a, gate_b, gate_c) = jnp.split(h @ w['w_in'], _split_points(IN_SIZES), axis=-1)

    fg = lb + (1.0 - lb) * jax.nn.sigmoid(hf.astype(f32))
    o_h, hg_s = gla_chunked(
        jax.nn.silu(hq.astype(f32)).reshape(b, L, HG_HEADS, HG_DK),
        (1.0 - fg).reshape(b, L, HG_HEADS, HG_DK),
        hi.reshape(b, L, HG_HEADS, HG_DV),
        jnp.log(fg).reshape(b, L, HG_HEADS, HG_DK),
        hg_s0)
    og = jax.nn.silu(hog.astype(f32)).reshape(b, L, HG_HEADS, HG_DV)
    o_a = (rmsnorm(o_h, w['hgrn_norm']) * og).reshape(b, L, -1).astype(x.dtype)

    qd = rope(rmsnorm(dq.reshape(b, L, DA_HEADS, 2, DA_DQK), w['diff_qk_norm'][0]), pos)
    kd = rope(rmsnorm(dk.reshape(b, L, DA_HEADS, 2, DA_DQK), w['diff_qk_norm'][1]), pos)
    vd = dv.reshape(b, L, DA_HEADS, DA_DV)
    if past_k is None:
        k_all, v_all = kd, vd
    else:
        k_all = jnp.concatenate([past_k.reshape(b, -1, DA_HEADS, 2, DA_DQK).astype(kd.dtype), kd], axis=1)
        v_all = jnp.concatenate([past_v.reshape(b, -1, DA_HEADS, DA_DV).astype(vd.dtype), vd], axis=1)
    lam_init = 0.8 - 0.6 * math.exp(-0.3 * layer_idx)
    lp = w['diff_lambda'].astype(f32)
    lam = jnp.exp(jnp.sum(lp[0] * lp[1])) - jnp.exp(jnp.sum(lp[2] * lp[3])) + lam_init
    o_d = diff_attention(qd, k_all, v_all, lam, pos, jnp.arange(k_all.shape[1]))
    o_b = (rmsnorm(o_d, w['diff_subln']) * (1.0 - lam_init)).reshape(b, L, -1).astype(x.dtype)

    qkv, gd_buf_new = causal_dwconv(jnp.concatenate([gq, gk, gv], axis=-1), gd_buf, w['gdn_conv'])
    qkv = jax.nn.silu(qkv)
    cq, ck, cv = jnp.split(qkv, [GD_HEADS * GD_DK, 2 * GD_HEADS * GD_DK], axis=-1)
    q_g = l2norm(cq.reshape(b, L, GD_HEADS, GD_DK)) * (GD_DK ** -0.5)
    k_g = l2norm(ck.reshape(b, L, GD_HEADS, GD_DK))
    v_g = cv.reshape(b, L, GD_HEADS, GD_DV)
    beta = jax.nn.sigmoid(gb.astype(f32))
    loga = -jnp.exp(w['gdn_a_log'].astype(f32)) * jax.nn.softplus(ga.astype(f32) + w['gdn_dt_bias'].astype(f32))
    o_g, gd_s = gdn_chunked(q_g, k_g, v_g, beta, loga, gd_s0)
    oz = jax.nn.silu(gz.astype(f32)).reshape(b, L, GD_HEADS, GD_DV)
    o_c = (rmsnorm(o_g, w['gdn_norm']) * oz).reshape(b, L, -1).astype(x.dtype)

    mix = (jax.nn.sigmoid(gate_a) * (o_a @ w['w_branch_a'])
           + jax.nn.sigmoid(gate_b) * (o_b @ w['w_branch_b'])
           + jax.nn.sigmoid(gate_c) * (o_c @ w['w_branch_c']))
    x = x + mix @ w['w_out']

    h2 = rmsnorm(x, w['ln_ffn'])
    g_ff, u_ff = jnp.split(h2 @ w['w_up'], 2, axis=-1)
    g_ff, ffn_buf_new = causal_dwconv(g_ff, ffn_buf, w['ffn_conv'])
    x = x + (jax.nn.silu(g_ff) * u_ff) @ w['w_down']

    k_rows = kd.reshape(b, L, DA_HEADS, 2 * DA_DQK)
    return x, hg_s, k_rows, vd, gd_s, gd_buf_new, ffn_buf_new


def setup_inputs(seed: int = 0) -> dict:
    key = jax.random.key(seed)
    ks = jax.random.split(key, 32)
    f32 = jnp.float32
    n_pages = PAST_LEN // PAGE_SIZE
    n_used = DEC_BATCH * n_pages
    n_pool = n_used + n_used // 4

    def nrm(k, shape, s):
        return jax.random.normal(k, shape, f32) * s

    dt = jnp.exp(jax.random.uniform(ks[18], (DEPTH, GD_HEADS), f32, math.log(1e-3), math.log(1e-1)))
    return {
        'x_prompt': nrm(ks[0], (BATCH, SEQ, D_MODEL), 1.0),
        'x_sample': nrm(ks[1], (DEC_BATCH, DEC_SEQ, D_MODEL), 1.0),
        'state_hgrn': nrm(ks[2], (DEPTH, DEC_BATCH, HG_HEADS, HG_DK, HG_DV), 0.1),
        'cache_k': nrm(ks[3], (DEPTH, n_pool, PAGE_SIZE, DA_HEADS, 2 * DA_DQK), 1.0),
        'cache_v': nrm(ks[4], (DEPTH, n_pool, PAGE_SIZE, DA_HEADS, DA_DV), 1.0),
        'state_gdn': nrm(ks[5], (DEPTH, DEC_BATCH, GD_HEADS, GD_DK, GD_DV), 0.1),
        'state_gdn_conv': nrm(ks[6], (DEPTH, DEC_BATCH, GD_CONV - 1, GD_CONV_CH), 1.0),
        'state_ffn_conv': nrm(ks[7], (DEPTH, DEC_BATCH, FFN_CONV - 1, D_FF), 1.0),
        'page_table': jax.random.permutation(ks[8], n_pool)[:n_used].reshape(DEC_BATCH, n_pages).astype(jnp.int32),
        'ln_mix': 1.0 + nrm(ks[9], (DEPTH, D_MODEL), 0.02),
        'w_in': nrm(ks[10], (DEPTH, D_MODEL, N_IN), D_MODEL ** -0.5),
        'hgrn_lb': 1.0 + nrm(ks[11], (DEPTH, HG_HEADS * HG_DK), 0.1),
        'hgrn_norm': 1.0 + nrm(ks[12], (DEPTH, HG_DV), 0.02),
        'diff_qk_norm': 1.0 + nrm(ks[13], (DEPTH, 2, DA_DQK), 0.02),
        'diff_lambda': nrm(ks[14], (DEPTH, 4, DA_DQK), 0.1),
        'diff_subln': 1.0 + nrm(ks[15], (DEPTH, DA_DV), 0.02),
        'gdn_conv': nrm(ks[16], (DEPTH, GD_CONV, GD_CONV_CH), GD_CONV ** -0.5),
        'gdn_a_log': jnp.log(jax.random.uniform(ks[17], (DEPTH, GD_HEADS), f32, 1.0, 16.0)),
        'gdn_dt_bias': dt + jnp.log(-jnp.expm1(-dt)),
        'gdn_norm': 1.0 + nrm(ks[19], (DEPTH, GD_DV), 0.02),
        'w_branch_a': nrm(ks[20], (DEPTH, HG_HEADS * HG_DV, D_MODEL), (HG_HEADS * HG_DV) ** -0.5),
        'w_branch_b': nrm(ks[21], (DEPTH, DA_HEADS * DA_DV, D_MODEL), (DA_HEADS * DA_DV) ** -0.5),
        'w_branch_c': nrm(ks[22], (DEPTH, GD_HEADS * GD_DV, D_MODEL), (GD_HEADS * GD_DV) ** -0.5),
        'w_out': nrm(ks[23], (DEPTH, D_MODEL, D_MODEL), D_MODEL ** -0.5),
        'ln_ffn': 1.0 + nrm(ks[24], (DEPTH, D_MODEL), 0.02),
        'w_up': nrm(ks[25], (DEPTH, D_MODEL, 2 * D_FF), D_MODEL ** -0.5),
        'ffn_conv': nrm(ks[26], (DEPTH, FFN_CONV, D_FF), FFN_CONV ** -0.5),
        'w_down': nrm(ks[27], (DEPTH, D_FF, D_MODEL), D_FF ** -0.5),
    }


def reference(x_prompt, x_sample, state_hgrn, cache_k, cache_v, state_gdn, state_gdn_conv,
              state_ffn_conv, page_table, ln_mix, w_in, hgrn_lb, hgrn_norm, diff_qk_norm,
              diff_lambda, diff_subln, gdn_conv, gdn_a_log, gdn_dt_bias, gdn_norm, w_branch_a,
              w_branch_b, w_branch_c, w_out, ln_ffn, w_up, ffn_conv, w_down):
    f32 = jnp.float32
    lb_w = jax.nn.softmax(hgrn_lb.astype(f32), axis=0)
    lbs = jnp.cumsum(lb_w, axis=0) - lb_w[0]

    bp = x_prompt.shape[0]
    yp, ys = x_prompt, x_sample
    p_hgrn, p_k, p_v, p_gdn, p_gconv, p_fconv = [], [], [], [], [], []
    s_hgrn, s_k, s_v, s_gdn, s_gconv, s_fconv = [], [], [], [], [], []
    for l in range(DEPTH):
        w = {'ln_mix': ln_mix[l], 'w_in': w_in[l], 'hgrn_norm': hgrn_norm[l],
             'diff_qk_norm': diff_qk_norm[l], 'diff_lambda': diff_lambda[l],
             'diff_subln': diff_subln[l], 'gdn_conv': gdn_conv[l], 'gdn_a_log': gdn_a_log[l],
             'gdn_dt_bias': gdn_dt_bias[l], 'gdn_norm': gdn_norm[l],
             'w_branch_a': w_branch_a[l], 'w_branch_b': w_branch_b[l], 'w_branch_c': w_branch_c[l],
             'w_out': w_out[l], 'ln_ffn': ln_ffn[l], 'w_up': w_up[l], 'ffn_conv': ffn_conv[l],
             'w_down': w_down[l]}
        yp, hs, kr, vr, gs, gbuf, fbuf = decoder_layer(
            yp, 0, l, lbs[l],
            jnp.zeros((bp, HG_HEADS, HG_DK, HG_DV), f32), None, None,
            jnp.zeros((bp, GD_HEADS, GD_DK, GD_DV), f32),
            jnp.zeros((bp, GD_CONV - 1, GD_CONV_CH), yp.dtype),
            jnp.zeros((bp, FFN_CONV - 1, D_FF), yp.dtype), w)
        p_hgrn.append(hs); p_k.append(kr); p_v.append(vr)
        p_gdn.append(gs); p_gconv.append(gbuf); p_fconv.append(fbuf)
        past_k = cache_k[l, page_table]
        past_v = cache_v[l, page_table]
        ys, hs, kr, vr, gs, gbuf, fbuf = decoder_layer(
            ys, PAST_LEN, l, lbs[l], state_hgrn[l], past_k, past_v,
            state_gdn[l], state_gdn_conv[l], state_ffn_conv[l], w)
        s_hgrn.append(hs); s_k.append(kr); s_v.append(vr)
        s_gdn.append(gs); s_gconv.append(gbuf); s_fconv.append(fbuf)

    return (yp, ys,
            jnp.stack(p_hgrn), jnp.stack(p_k), jnp.stack(p_v), jnp.stack(p_gdn),
            jnp.stack(p_gconv), jnp.stack(p_fconv),
            jnp.stack(s_hgrn), jnp.stack(s_k), jnp.stack(s_v), jnp.stack(s_gdn),
            jnp.stack(s_gconv), jnp.stack(s_fconv))
```

```python
import functools
import math

import jax
import jax.numpy as jnp
from jax import lax
from jax.experimental import pallas as pl
from jax.experimental.pallas import tpu as pltpu

f32 = jnp.float32
bf16 = jnp.bfloat16

HEADS = 4
HD = 128
DQK = 64
HW = HEADS * HD
ROPE_THETA = 10000.0
GD_CONV = 4
FFN_CONV = 3
EPS = 1e-6
NEG_INF = -1e30

SUBLANES = 8
LANES = 128
VMEM_LIMIT = 56 * 1024 * 1024

GLA_BLOCK = 16
GDN_CHUNK = 64
GATE_COLS = 3 * 1024
GRP0 = GATE_COLS // HW


def _grp(n):
    return GRP0 + n


def _cparams(sem):
    return pltpu.CompilerParams(dimension_semantics=sem, vmem_limit_bytes=VMEM_LIMIT)


def _split2(x):
    hi = x.astype(bf16)
    lo = (x - hi.astype(f32)).astype(bf16)
    return hi, lo


def _split3(x):
    hi = x.astype(bf16)
    r = x - hi.astype(f32)
    mid = r.astype(bf16)
    lo = (r - mid.astype(f32)).astype(bf16)
    return hi, mid, lo


def _dot(a, b):
    return jnp.dot(a, b, preferred_element_type=f32)


def _dot_nt(a, b):
    return lax.dot_general(a, b, (((1,), (1,)), ((), ())), preferred_element_type=f32)


def _dot_tn(a, b):
    return lax.dot_general(a, b, (((0,), (0,)), ((), ())), preferred_element_type=f32)


def _silu(x):
    return x * jax.nn.sigmoid(x)


def _blk_tri(n, blk):
    r = lax.broadcasted_iota(jnp.int32, (n, n), 0)
    c = lax.broadcasted_iota(jnp.int32, (n, n), 1)
    sh = int(math.log2(blk))
    same = jnp.where((r >> sh) == (c >> sh), 1.0, 0.0)
    return jnp.where(c <= r, same, 0.0).astype(bf16)


def _blk_cumsum(tri, x):
    hi, mid, lo = _split3(x)
    return _dot(tri, hi) + _dot(tri, mid) + _dot(tri, lo)


def _proj_kernel(x_ref, g_ref, w_ref, wsm_ref, o_ref, osm_ref, h_s):
    @pl.when(pl.program_id(1) == 0)
    def _():
        x = x_ref[...]
        ms = jnp.mean(x * x, axis=-1, keepdims=True)
        hb = (x * lax.rsqrt(ms + EPS) * g_ref[...]).astype(bf16)
        h_s[...] = hb
        osm_ref[...] = _dot(hb, wsm_ref[...])

    o_ref[...] = _dot(h_s[...], w_ref[...])


def _proj_in(x, g, w_main, w_small):
    T, D = x.shape
    N = w_main.shape[1]
    tm = min(1024, T)
    tn = 512
    return pl.pallas_call(
        _proj_kernel,
        out_shape=(jax.ShapeDtypeStruct((T, N), f32), jax.ShapeDtypeStruct((T, LANES), f32)),
        grid=(T // tm, N // tn),
        in_specs=[
            pl.BlockSpec((tm, D), lambda i, j: (i, 0)),
            pl.BlockSpec((1, D), lambda i, j: (0, 0)),
            pl.BlockSpec((D, tn), lambda i, j: (0, j)),
            pl.BlockSpec((D, LANES), lambda i, j: (0, 0)),
        ],
        out_specs=(
            pl.BlockSpec((tm, tn), lambda i, j: (i, j)),
            pl.BlockSpec((tm, LANES), lambda i, j: (i, 0)),
        ),
        scratch_shapes=[pltpu.VMEM((tm, D), bf16)],
        compiler_params=_cparams(("arbitrary", "arbitrary")),
        name="proj_in",
    )(x, g, w_main, w_small)


def _hgrn_kernel(hq_ref, hf_ref, hi_ref, hog_ref, lb_ref, gn_ref, s0_ref, o_ref, sf_ref,
                 st_s, q_s, k_s, v_s, g_s, o_s, *, tr, R):
    t = pl.program_id(1)

    @pl.when(t == 0)
    def _():
        for h in range(HEADS):
            st_s[h] = s0_ref[h].T

    trp = q_s.shape[0]

    def pad(x):
        if trp == tr:
            return x
        return jnp.concatenate([x, jnp.zeros((trp - tr, x.shape[1]), x.dtype)], axis=0)

    hq = hq_ref[...]
    lb = lb_ref[...]
    fg = lb + (1.0 - lb) * jax.nn.sigmoid(hf_ref[...])
    q_s[...] = pad(_silu(hq))
    k_s[...] = pad(1.0 - fg)
    v_s[...] = pad(hi_ref[...])
    g_s[...] = _blk_cumsum(_blk_tri(trp, R), pad(jnp.log(fg)))

    row = lax.broadcasted_iota(jnp.int32, (R, 1), 0)

    def blk(i, carry):
        r0 = pl.multiple_of(i * R, R)
        for h in range(HEADS):
            c0, c1 = h * HD, (h + 1) * HD
            gb = g_s[pl.ds(r0, R), c0:c1]
            qb = q_s[pl.ds(r0, R), c0:c1]
            kb = k_s[pl.ds(r0, R), c0:c1]
            vb = v_s[pl.ds(r0, R), c0:c1]
            st = st_s[h]
            o = _dot_nt((qb * jnp.exp(gb)).astype(bf16), st.astype(bf16))
            for s in range(R):
                e = jnp.exp(jnp.minimum(gb - gb[s:s + 1, :], 0.0))
                w = jnp.sum(qb * e * kb[s:s + 1, :], axis=-1, keepdims=True)
                o = o + jnp.where(row >= s, w, 0.0) * vb[s:s + 1, :]
            gl = gb[R - 1:R, :]
            kd = (kb * jnp.exp(gl - gb)).astype(bf16)
            st_s[h] = jnp.exp(gl) * st + _dot_tn(vb.astype(bf16), kd)
            o_s[pl.ds(r0, R), c0:c1] = o
        return carry

    lax.fori_loop(0, trp // R, blk, 0)

    og = _silu(hog_ref[...])
    gn = gn_ref[...]
    for h in range(HEADS):
        c0, c1 = h * HD, (h + 1) * HD
        oh = o_s[0:tr, c0:c1]
        ms = jnp.mean(oh * oh, axis=-1, keepdims=True)
        o_ref[:, c0:c1] = (oh * lax.rsqrt(ms + EPS) * gn * og[:, c0:c1]).astype(o_ref.dtype)

    @pl.when(t == pl.num_programs(1) - 1)
    def _():
        for h in range(HEADS):
            sf_ref[h] = st_s[h].T


def _hgrn(proj, lb, gn, s0, B, L):
    T = B * L
    tr = min(256, L)
    R = GLA_BLOCK
    trp = max(tr, R)
    nt = L // tr
    col = lambda n: pl.BlockSpec((tr, HW), lambda b, t, n=n: (b * nt + t, _grp(n)))
    state = pl.BlockSpec((None, HEADS, HD, HD), lambda b, t: (b, 0, 0, 0))
    return pl.pallas_call(
        functools.partial(_hgrn_kernel, tr=tr, R=R),
        out_shape=(jax.ShapeDtypeStruct((T, HW), bf16),
                   jax.ShapeDtypeStruct((B, HEADS, HD, HD), f32)),
        grid=(B, nt),
        in_specs=[col(0), col(1), col(2), col(3),
                  pl.BlockSpec((1, HW), lambda b, t: (0, 0)),
                  pl.BlockSpec((1, HD), lambda b, t: (0, 0)),
                  state],
        out_specs=(pl.BlockSpec((tr, HW), lambda b, t: (b * nt + t, 0)), state),
        scratch_shapes=[pltpu.VMEM((HEADS, HD, HD), f32)] + [pltpu.VMEM((trp, HW), f32)] * 5,
        compiler_params=_cparams(("arbitrary", "arbitrary")),
        name="hgrn2",
    )(proj, proj, proj, proj, lb, gn, s0)


def _attn_prep_kernel(dq_ref, dk_ref, dv_ref, wq_ref, wk_ref, cos_ref, sin_ref, bd_ref,
                      q_ref, k_ref, kb_ref, v_ref, vb_ref):
    bd = bd_ref[...]
    cos = cos_ref[...]
    sin = sin_ref[...]
    lane = lax.broadcasted_iota(jnp.int32, (1, HW), 1)
    first_half = (lane & (DQK - 1)) < DQK // 2

    def norm_rope(x, w):
        hi, lo = _split2(x * x)
        ms = _dot(hi, bd) + _dot(lo, bd)
        xn = x * lax.rsqrt(ms + EPS) * w
        rot = jnp.where(first_half, pltpu.roll(xn, HW - DQK // 2, 1), pltpu.roll(xn, DQK // 2, 1))
        return xn * cos + rot * sin

    q_ref[...] = (norm_rope(dq_ref[...], wq_ref[...]) * (DQK ** -0.5)).astype(q_ref.dtype)
    k = norm_rope(dk_ref[...], wk_ref[...])
    k_ref[...] = k
    kb_ref[...] = k.astype(bf16)
    v = dv_ref[...]
    v_ref[...] = v
    vb_ref[...] = v.astype(bf16)


def _attn_prep(proj, wq, wk, cos, sin, bd, q_dtype):
    T = proj.shape[0]
    LT = cos.shape[0]
    tr = min(512, LT)
    nt = LT // tr
    col = lambda n: pl.BlockSpec((tr, HW), lambda i, n=n: (i, _grp(n)))
    vec = pl.BlockSpec((1, HW), lambda i: (0, 0))
    tab = pl.BlockSpec((tr, HW), lambda i: (i % nt, 0))
    out = pl.BlockSpec((tr, HW), lambda i: (i, 0))
    return pl.pallas_call(
        _attn_prep_kernel,
        out_shape=(jax.ShapeDtypeStruct((T, HW), q_dtype),
                   jax.ShapeDtypeStruct((T, HW), f32), jax.ShapeDtypeStruct((T, HW), bf16),
                   jax.ShapeDtypeStruct((T, HW), f32), jax.ShapeDtypeStruct((T, HW), bf16)),
        grid=(T // tr,),
        in_specs=[col(4), col(5), col(6), vec, vec, tab, tab,
                  pl.BlockSpec((HW, HW), lambda i: (0, 0))],
        out_specs=(out, out, out, out, out),
        compiler_params=_cparams(("arbitrary",)),
        name="attn_prep",
    )(proj, proj, proj, wq, wk, cos, sin, bd)


def _softmax_step(s, vb_of, m_prev, l_prev, acc_prev):
    m_new = jnp.maximum(m_prev, jnp.max(s, axis=-1, keepdims=True))
    p = jnp.exp(s - m_new)
    alpha = jnp.exp(m_prev - m_new)
    l_new = alpha * l_prev + jnp.sum(p, axis=-1, keepdims=True)
    acc_new = alpha * acc_prev + vb_of(p)
    return m_new, l_new, acc_new


def _diff_finish(acc, l, n, lam, sub, lam_init):
    o = acc[:n] / l[:n] - lam * (acc[n:] / l[n:])
    ms = jnp.mean(o * o, axis=-1, keepdims=True)
    return o * lax.rsqrt(ms + EPS) * sub * (1.0 - lam_init)


def _flash_kernel(lam_ref, q_ref, k_ref, v_ref, sub_ref, o_ref, m_s, l_s, acc_s, *, tq, lam_init):
    qi = pl.program_id(2)
    q = q_ref[...]
    lane = lax.broadcasted_iota(jnp.int32, (1, HD), 1)
    zero = jnp.zeros_like(q)
    qq = jnp.concatenate([jnp.where(lane < DQK, q, zero), jnp.where(lane >= DQK, q, zero)], axis=0)
    m_s[...] = jnp.full(m_s.shape, NEG_INF, f32)
    l_s[...] = jnp.zeros(l_s.shape, f32)
    acc_s[...] = jnp.zeros(acc_s.shape, f32)

    def step(j, masked):
        r0 = pl.multiple_of(j * tq, tq)
        kb = k_ref[pl.ds(r0, tq), :]
        vb = v_ref[pl.ds(r0, tq), :]
        s = _dot_nt(qq, kb)
        if masked:
            r = lax.broadcasted_iota(jnp.int32, (2 * tq, 1), 0) & (tq - 1)
            c = lax.broadcasted_iota(jnp.int32, (1, tq), 1)
            s = jnp.where(c <= r, s, NEG_INF)
        m, l, acc = _softmax_step(s, lambda p: _dot(p.astype(bf16), vb),
                                  m_s[...], l_s[...], acc_s[...])
        m_s[...] = m
        l_s[...] = l
        acc_s[...] = acc

    def body(j, carry):
        step(j, False)
        return carry

    lax.fori_loop(0, qi, body, 0)
    step(qi, True)
    o_ref[...] = _diff_finish(acc_s[...], l_s[...], tq, lam_ref[...], sub_ref[...],
                              lam_init).astype(o_ref.dtype)


def _flash(q, k, v, lam, sub, B, L, lam_init):
    T = B * L
    tq = min(512, L)
    assert tq & (tq - 1) == 0
    nq = L // tq
    return pl.pallas_call(
        functools.partial(_flash_kernel, tq=tq, lam_init=lam_init),
        out_shape=jax.ShapeDtypeStruct((T, HW), bf16),
        grid=(B, HEADS, nq),
        in_specs=[
            pl.BlockSpec((1, HD), lambda b, h, i: (0, 0)),
            pl.BlockSpec((tq, HD), lambda b, h, i: (b * nq + i, h)),
            pl.BlockSpec((L, HD), lambda b, h, i: (b, h)),
            pl.BlockSpec((L, HD), lambda b, h, i: (b, h)),
            pl.BlockSpec((1, HD), lambda b, h, i: (0, 0)),
        ],
        out_specs=pl.BlockSpec((tq, HD), lambda b, h, i: (b * nq + i, h)),
        scratch_shapes=[pltpu.VMEM((2 * tq, 1), f32), pltpu.VMEM((2 * tq, 1), f32),
                        pltpu.VMEM((2 * tq, HD), f32)],
        compiler_params=_cparams(("arbitrary", "arbitrary", "arbitrary")),
        name="diff_flash",
    )(lam, q, k, v, sub)


def _paged_kernel(pt_ref, lam_ref, q_ref, kn_ref, vn_ref, sub_ref, *rest, P, lam_init):
    del pt_ref
    k_refs, v_refs = rest[:P], rest[P:2 * P]
    o_ref, qq_s, m_s, l_s, acc_s = rest[2 * P:]
    j = pl.program_id(1)
    n = q_ref.shape[0]
    nr = 2 * n * HEADS

    @pl.when(j == 0)
    def _():
        q = q_ref[...]
        lane = lax.broadcasted_iota(jnp.int32, (1, HD), 1)
        rows = []
        for h in range(HEADS):
            qh = q[:, h * HD:(h + 1) * HD]
            rows += [jnp.where(lane < DQK, qh, 0.0), jnp.where(lane >= DQK, qh, 0.0)]
        qq_s[...] = jnp.concatenate(rows, axis=0).astype(bf16)
        m_s[...] = jnp.full(m_s.shape, NEG_INF, f32)
        l_s[...] = jnp.zeros(l_s.shape, f32)
        acc_s[...] = jnp.zeros(acc_s.shape, f32)

    def process(kps, vps, causal):
        pr = kps[0].shape[0]
        qq = qq_s[...]
        ss = [_dot_nt(qq, kp[...].astype(bf16)) for kp in kps]
        s = ss[0] if len(ss) == 1 else jnp.concatenate(ss, axis=1)
        r = lax.broadcasted_iota(jnp.int32, (nr, 1), 0)
        c = lax.broadcasted_iota(jnp.int32, (1, pr * len(kps)), 1)
        s = jnp.where((c & (HEADS - 1)) == (r >> int(math.log2(2 * n))), s, NEG_INF)
        if causal:
            s = jnp.where((c >> int(math.log2(HEADS))) <= (r & (n - 1)), s, NEG_INF)

        def pv(p):
            out = None
            for i, vp in enumerate(vps):
                d = _dot(p[:, i * pr:(i + 1) * pr].astype(bf16), vp[...].astype(bf16))
                out = d if out is None else out + d
            return out

        m, l, acc = _softmax_step(s, pv, m_s[...], l_s[...], acc_s[...])
        m_s[...] = m
        l_s[...] = l
        acc_s[...] = acc

    process(k_refs, v_refs, False)

    @pl.when(j == pl.num_programs(1) - 1)
    def _():
        process([kn_ref], [vn_ref], True)
        lam = lam_ref[...]
        sub = sub_ref[...]
        for h in range(HEADS):
            r0, r1 = h * 2 * n, (h + 1) * 2 * n
            o_ref[:, h * HD:(h + 1) * HD] = _diff_finish(
                acc_s[r0:r1], l_s[r0:r1], n, lam, sub, lam_init).astype(o_ref.dtype)


def _paged_attn(page_table, lam, q3, k_new, v_new, sub, cache_k, cache_v, layer, lam_init):
    B, n, _ = q3.shape
    assert n & (n - 1) == 0
    pr = cache_k.shape[2]
    n_pages = page_table.shape[0] // B
    P = math.gcd(n_pages, 8)
    nsteps = n_pages // P
    nr = 2 * n * HEADS

    def page(i):
        return pl.BlockSpec((None, None, pr, HD),
                            lambda b, j, pt, i=i: (layer, pt[b * n_pages + j * P + i], 0, 0))

    per_b3 = lambda r, w: pl.BlockSpec((None, r, w), lambda b, j, pt: (b, 0, 0))
    vec = pl.BlockSpec((1, HD), lambda b, j, pt: (0, 0))
    return pl.pallas_call(
        functools.partial(_paged_kernel, P=P, lam_init=lam_init),
        out_shape=jax.ShapeDtypeStruct((B, n, HW), bf16),
        grid_spec=pltpu.PrefetchScalarGridSpec(
            num_scalar_prefetch=1,
            grid=(B, nsteps),
            in_specs=[vec, per_b3(n, HW), per_b3(pr, HD), per_b3(pr, HD), vec]
                     + [page(i) for i in range(P)] * 2,
            out_specs=per_b3(n, HW),
            scratch_shapes=[pltpu.VMEM((nr, HD), bf16),
                            pltpu.VMEM((nr, 1), f32),
                            pltpu.VMEM((nr, 1), f32),
                            pltpu.VMEM((nr, HD), f32)],
        ),
        compiler_params=_cparams(("arbitrary", "arbitrary")),
        name="paged_diff_attn",
    )(page_table, lam, q3, k_new, v_new, sub, *([cache_k] * P), *([cache_v] * P))


def _inv_unit_lower(a, n):
    hp = lax.Precision.HIGHEST
    r = lax.broadcasted_iota(jnp.int32, (n, n), 0)
    c = lax.broadcasted_iota(jnp.int32, (n, n), 1)
    b = -a
    p = jnp.where(r == c, 1.0, 0.0) + b
    k = 2
    while k < n:
        b = jnp.dot(b, b, precision=hp, preferred_element_type=f32)
        p = p + jnp.dot(p, b, precision=hp, preferred_element_type=f32)
        k *= 2
    return p


def _shift_rows(x, car, d):
    n = x.shape[0]
    xr = pltpu.roll(x, d, 0)
    cr = pltpu.roll(car, d, 0)
    row = lax.broadcasted_iota(jnp.int32, (SUBLANES, 1), 0)
    if n == SUBLANES:
        return jnp.where(row < d, cr, xr)
    return jnp.concatenate([jnp.where(row < d, cr, xr[:SUBLANES]), xr[SUBLANES:]], axis=0)


def _gdn_kernel(gq_ref, gk_ref, gv_ref, gz_ref, gba_ref, cw_ref, buf0_ref, alog_ref, dtb_ref,
                gn_ref, s0_ref, o_ref, sf_ref, bufo_ref,
                carry_s, st_s, q_s, k_s, v_s, g_s, b_s, o_s, *, tr, C):
    t = pl.program_id(1)
    trp = q_s.shape[0]

    @pl.when(t == 0)
    def _():
        carry_s[...] = buf0_ref[...]
        st_s[...] = s0_ref[...]

    conv = []
    for idx, ref in enumerate((gq_ref, gk_ref, gv_ref)):
        c0, c1 = idx * HW, (idx + 1) * HW
        x = ref[...]
        car = carry_s[:, c0:c1]
        y = x * cw_ref[GD_CONV - 1:GD_CONV, c0:c1]
        for d in range(1, GD_CONV):
            y = y + _shift_rows(x, car, d) * cw_ref[GD_CONV - 1 - d:GD_CONV - d, c0:c1]
        carry_s[:, c0:c1] = x[tr - SUBLANES:tr]
        conv.append(_silu(y))

    def l2n(x, scale):
        outs = []
        for h in range(HEADS):
            xh = x[:, h * HD:(h + 1) * HD]
            outs.append(xh * (lax.rsqrt(jnp.sum(xh * xh, axis=-1, keepdims=True) + EPS) * scale))
        return jnp.concatenate(outs, axis=1)

    def pad(x):
        if trp == tr:
            return x
        return jnp.concatenate([x, jnp.zeros((trp - tr, x.shape[1]), x.dtype)], axis=0)

    q_s[...] = pad(l2n(conv[0], HD ** -0.5))
    k_s[...] = pad(l2n(conv[1], 1.0))
    v_s[...] = pad(conv[2])
    gba = gba_ref[...]
    b_s[...] = pad(jax.nn.sigmoid(gba))
    xa = gba + dtb_ref[...]
    softplus = jnp.maximum(xa, 0.0) + jnp.log(1.0 + jnp.exp(-jnp.abs(xa)))
    loga = -jnp.exp(alog_ref[...]) * softplus
    g_s[...] = _blk_cumsum(_blk_tri(trp, C), pad(loga))

    rr = lax.broadcasted_iota(jnp.int32, (C, C), 0)
    cc = lax.broadcasted_iota(jnp.int32, (C, C), 1)

    def chunk(ci, carry):
        r0 = pl.multiple_of(ci * C, C)
        g = g_s[pl.ds(r0, C), :]
        gt = g.T
        bt = b_s[pl.ds(r0, C), :]
        for h in range(HEADS):
            c0, c1 = h * HD, (h + 1) * HD
            qh = q_s[pl.ds(r0, C), c0:c1]
            kh = k_s[pl.ds(r0, C), c0:c1]
            vh = v_s[pl.ds(r0, C), c0:c1]
            gcol = g[:, HEADS + h:HEADS + h + 1]
            grow = gt[HEADS + h:HEADS + h + 1, :]
            bcol = bt[:, h:h + 1]
            dec = jnp.where(rr >= cc, jnp.exp(jnp.minimum(gcol - grow, 0.0)), 0.0)
            kq = jnp.concatenate([qh, kh], axis=0).astype(bf16)
            qk_kk = _dot_nt(kq, kh.astype(bf16))
            a = jnp.where(rr > cc, bcol * dec * qk_kk[C:], 0.0)
            tm = _inv_unit_lower(a, C)
            qkd = (qk_kk[:C] * dec).astype(bf16)
            eg = jnp.exp(gcol)
            gl = gcol[C - 1:C, :]
            kdec = (kh * jnp.exp(gl - gcol)).astype(bf16)
            st = st_s[h]
            qs_ks = _dot(kq, st.astype(bf16))
            rhs = bcol * (vh - eg * qs_ks[C:])
            u = _dot(tm.astype(bf16), rhs.astype(bf16)).astype(bf16)
            o_s[pl.ds(r0, C), c0:c1] = eg * qs_ks[:C] + _dot(qkd, u)
            st_s[h] = jnp.exp(gl) * st + _dot_tn(kdec, u)
        return carry

    lax.fori_loop(0, trp // C, chunk, 0)

    oz = _silu(gz_ref[...])
    gn = gn_ref[...]
    for h in range(HEADS):
        c0, c1 = h * HD, (h + 1) * HD
        oh = o_s[0:tr, c0:c1]
        ms = jnp.mean(oh * oh, axis=-1, keepdims=True)
        o_ref[:, c0:c1] = (oh * lax.rsqrt(ms + EPS) * gn * oz[:, c0:c1]).astype(o_ref.dtype)

    @pl.when(t == pl.num_programs(1) - 1)
    def _():
        sf_ref[...] = st_s[...]
        bufo_ref[...] = carry_s[...]


def _gdn(proj, gba, cw, buf0, alog, dtb, gn, s0, B, L):
    T = B * L
    tr = min(256, L)
    C = GDN_CHUNK
    trp = max(tr, C)
    nt = L // tr
    CH = 3 * HW
    col = lambda n: pl.BlockSpec((tr, HW), lambda b, t, n=n: (b * nt + t, _grp(n)))
    state = pl.BlockSpec((None, HEADS, HD, HD), lambda b, t: (b, 0, 0, 0))
    buf = pl.BlockSpec((None, SUBLANES, CH), lambda b, t: (b, 0, 0))
    vec = pl.BlockSpec((1, LANES), lambda b, t: (0, 0))
    return pl.pallas_call(
        functools.partial(_gdn_kernel, tr=tr, C=C),
        out_shape=(jax.ShapeDtypeStruct((T, HW), bf16),
                   jax.ShapeDtypeStruct((B, HEADS, HD, HD), f32),
                   jax.ShapeDtypeStruct((B, SUBLANES, CH), f32)),
        grid=(B, nt),
        in_specs=[col(7), col(8), col(9), col(10),
                  pl.BlockSpec((tr, LANES), lambda b, t: (b * nt + t, 0)),
                  pl.BlockSpec((GD_CONV, CH), lambda b, t: (0, 0)),
                  buf, vec, vec, vec, state],
        out_specs=(pl.BlockSpec((tr, HW), lambda b, t: (b * nt + t, 0)), state, buf),
        scratch_shapes=[pltpu.VMEM((SUBLANES, CH), f32), pltpu.VMEM((HEADS, HD, HD), f32),
                        pltpu.VMEM((trp, HW), f32), pltpu.VMEM((trp, HW), f32),
                        pltpu.VMEM((trp, HW), f32), pltpu.VMEM((trp, LANES), f32),
                        pltpu.VMEM((trp, LANES), f32), pltpu.VMEM((trp, HW), f32)],
        compiler_params=_cparams(("arbitrary", "arbitrary")),
        name="gated_deltanet",
    )(proj, proj, proj, proj, gba, cw, buf0, alog, dtb, gn, s0)


def _merge_kernel(x_ref, ga_ref, gb_ref, gc_ref, oa_ref, ob_ref, oc_ref,
                  wa_ref, wb_ref, wc_ref, wo_ref, o_ref):
    mix = (jax.nn.sigmoid(ga_ref[...]) * _dot(oa_ref[...], wa_ref[...])
           + jax.nn.sigmoid(gb_ref[...]) * _dot(ob_ref[...], wb_ref[...])
           + jax.nn.sigmoid(gc_ref[...]) * _dot(oc_ref[...], wc_ref[...]))
    o_ref[...] = x_ref[...] + _dot(mix.astype(bf16), wo_ref[...])


def _merge(x, proj, oa, ob, oc, wa, wb, wc, wo):
    T, D = x.shape
    tm = min(512, T)
    row = lambda w: pl.BlockSpec((tm, w), lambda i: (i, 0))
    gate = lambda n: pl.BlockSpec((tm, D), lambda i, n=n: (i, n))
    full = lambda a: pl.BlockSpec(a.shape, lambda i: (0, 0))
    return pl.pallas_call(
        _merge_kernel,
        out_shape=jax.ShapeDtypeStruct((T, D), f32),
        grid=(T // tm,),
        in_specs=[row(D), gate(0), gate(1), gate(2), row(HW), row(HW), row(HW),
                  full(wa), full(wb), full(wc), full(wo)],
        out_specs=row(D),
        compiler_params=_cparams(("arbitrary",)),
        name="merge_out",
    )(x, proj, proj, proj, oa, ob, oc, wa, wb, wc, wo)


def _ffn_kernel(*refs, tm, tps, per_row_init):
    if per_row_init:
        (x_ref, g_ref, wg_ref, wu_ref, wd_ref, cw_ref, in1_ref, in2_ref,
         o_ref, gt_ref, h_s, acc_s) = refs
    else:
        (x_ref, g_ref, wg_ref, wu_ref, wd_ref, cw_ref, init_ref,
         o_ref, gt_ref, h_s, acc_s, carry_s) = refs
    i = pl.program_id(0)
    j = pl.program_id(1)

    @pl.when(j == 0)
    def _():
        x = x_ref[...]
        ms = jnp.mean(x * x, axis=-1, keepdims=True)
        h_s[...] = (x * lax.rsqrt(ms + EPS) * g_ref[...]).astype(bf16)
        acc_s[...] = jnp.zeros(acc_s.shape, f32)

    h = h_s[...]
    g = _dot(h, wg_ref[...])
    u = _dot(h, wu_ref[...])
    y = g * cw_ref[FFN_CONV - 1:FFN_CONV, :]
    if per_row_init:
        pos = lax.broadcasted_iota(jnp.int32, (tm, 1), 0) & (SUBLANES - 1)
        y = y + jnp.where(pos < 1, in1_ref[...], pltpu.roll(g, 1, 0)) * cw_ref[1:2, :]
        y = y + jnp.where(pos < 2, in2_ref[...], pltpu.roll(g, 2, 0)) * cw_ref[0:1, :]
        gt_ref[...] = g
    else:
        @pl.when(i % tps == 0)
        def _():
            carry_s[j] = init_ref[...]

        car = carry_s[j]
        for d in range(1, FFN_CONV):
            y = y + _shift_rows(g, car, d) * cw_ref[FFN_CONV - 1 - d:FFN_CONV - d, :]
        tail = g[tm - SUBLANES:tm]
        carry_s[j] = tail
        gt_ref[...] = tail
    acc_s[...] += _dot((_silu(y) * u).astype(bf16), wd_ref[...])

    @pl.when(j == pl.num_programs(1) - 1)
    def _():
        o_ref[...] = x_ref[...] + acc_s[...]


def _ffn(x, g, w_up, w_down, cw, buf, B, L):
    T, D = x.shape
    F = w_down.shape[0]
    tf = 256
    nj = F // tf
    if L > SUBLANES:
        tm = min(1024, L)
        assert L % tm == 0
        tps = L // tm
        init = jnp.pad(buf, ((0, 0), (SUBLANES - (FFN_CONV - 1), 0), (0, 0)))
        extra_in = [init]
        extra_specs = [pl.BlockSpec((None, SUBLANES, tf), lambda i, j: (i // tps, 0, j))]
        gt_spec = pl.BlockSpec((SUBLANES, tf), lambda i, j: (i, j))
        scratch = [pltpu.VMEM((nj, SUBLANES, tf), f32)]
        per_row = False
    else:
        assert L == SUBLANES
        tm, tps = T, 1
        z = jnp.zeros((B, L, F), f32)
        in1 = z.at[:, 0].set(buf[:, 1]).reshape(T, F)
        in2 = z.at[:, 0].set(buf[:, 0]).at[:, 1].set(buf[:, 1]).reshape(T, F)
        extra_in = [in1, in2]
        extra_specs = [pl.BlockSpec((tm, tf), lambda i, j: (i, j))] * 2
        gt_spec = pl.BlockSpec((tm, tf), lambda i, j: (i, j))
        scratch = []
        per_row = True
    y, tails = pl.pallas_call(
        functools.partial(_ffn_kernel, tm=tm, tps=tps, per_row_init=per_row),
        out_shape=(jax.ShapeDtypeStruct((T, D), f32),
                   jax.ShapeDtypeStruct((B * tps * SUBLANES, F), f32)),
        grid=(T // tm, nj),
        in_specs=[pl.BlockSpec((tm, D), lambda i, j: (i, 0)),
                  pl.BlockSpec((1, D), lambda i, j: (0, 0)),
                  pl.BlockSpec((D, tf), lambda i, j: (0, j)),
                  pl.BlockSpec((D, tf), lambda i, j: (0, nj + j)),
                  pl.BlockSpec((tf, D), lambda i, j: (j, 0)),
                  pl.BlockSpec((FFN_CONV, tf), lambda i, j: (0, j))] + extra_specs,
        out_specs=(pl.BlockSpec((tm, D), lambda i, j: (i, 0)), gt_spec),
        scratch_shapes=[pltpu.VMEM((tm, D), bf16), pltpu.VMEM((tm, D), f32)] + scratch,
        compiler_params=_cparams(("arbitrary", "arbitrary")),
        name="conv_ffn",
    )(x, g, w_up, w_up, w_down, cw, *extra_in)
    return y, tails.reshape(B, tps, SUBLANES, F)[:, tps - 1]


def _rope_tables(pos):
    inv = 1.0 / (ROPE_THETA ** (jnp.arange(0, DQK, 2, dtype=f32) / DQK))
    ang = pos.astype(f32)[:, None] * inv[None, :]
    cos, sin = jnp.cos(ang), jnp.sin(ang)
    rep = HW // DQK
    return (jnp.tile(jnp.concatenate([cos, cos], axis=-1), (1, rep)),
            jnp.tile(jnp.concatenate([-sin, sin], axis=-1), (1, rep)))


def _layer(x, B, L, pos0, layer, w, hg_s0, gd_s0, gd_buf, ffn_buf, paged):
    T = B * L
    proj, gba = _proj_in(x, w['ln_mix'], w['w_in_main'], w['w_in_small'])

    o_a, hg_s = _hgrn(proj, w['lb'], w['hgrn_norm'], hg_s0, B, L)

    pos = pos0 + jnp.arange(L)
    cos, sin = _rope_tables(pos)
    lam_init = 0.8 - 0.6 * math.exp(-0.3 * layer)
    if paged is None:
        q, k_rows, k_b, v_rows, v_b = _attn_prep(proj, w['wq'], w['wk'], cos, sin, w['bd'], bf16)
        o_b = _flash(q, k_b, v_b, w['lam'], w['diff_subln'], B, L, lam_init)
    else:
        page_table, cache_k, cache_v = paged
        pg = cache_k.shape[2] // HEADS
        q, k_rows, _, v_rows, _ = _attn_prep(proj, w['wq'], w['wk'], jnp.tile(cos, (B, 1)),
                                             jnp.tile(sin, (B, 1)), w['bd'], f32)
        padp = lambda a: jnp.pad(a.reshape(B, L, HEADS, HD),
                                 ((0, 0), (0, pg - L), (0, 0), (0, 0))).reshape(B, pg * HEADS, HD)
        o_b = _paged_attn(page_table, w['lam'], q.reshape(B, L, HW), padp(k_rows), padp(v_rows),
                          w['diff_subln'], cache_k, cache_v, layer, lam_init).reshape(T, HW)

    o_c, gd_s, gd_tail = _gdn(proj, gba, w['gdn_conv'], gd_buf, w['alog'], w['dtb'],
                              w['gdn_norm'], gd_s0, B, L)

    x = _merge(x, proj, o_a, o_b, o_c, w['w_branch_a'], w['w_branch_b'], w['w_branch_c'], w['w_out'])
    x, ffn_tail = _ffn(x, w['ln_ffn'], w['w_up'], w['w_down'], w['ffn_conv'], ffn_buf, B, L)

    gd_buf_new = gd_tail[:, SUBLANES - (GD_CONV - 1):]
    ffn_buf_new = ffn_tail.reshape(B, SUBLANES, -1)[:, SUBLANES - (FFN_CONV - 1):]
    return (x, hg_s, k_rows.reshape(B, L, HEADS, HD), v_rows.reshape(B, L, HEADS, HD),
            gd_s, gd_buf_new, ffn_buf_new)


def _layer_weights(l, lbs, ln_mix, w_in, hgrn_norm, diff_qk_norm, diff_lambda, diff_subln, gdn_conv,
                   gdn_a_log, gdn_dt_bias, gdn_norm, w_branch_a, w_branch_b, w_branch_c, w_out,
                   ln_ffn, w_up, ffn_conv, w_down):
    n_grp = 11 * HW
    wi = w_in[l]
    lp = diff_lambda[l].astype(f32)
    lam_init = 0.8 - 0.6 * math.exp(-0.3 * l)
    lam = jnp.exp(jnp.sum(lp[0] * lp[1])) - jnp.exp(jnp.sum(lp[2] * lp[3])) + lam_init
    lane_pad = lambda v, off: jnp.zeros((1, LANES), f32).at[0, off:off + HEADS].set(v)
    r = jnp.arange(HW) // DQK
    return {
        'ln_mix': ln_mix[l][None, :],
        'w_in_main': jnp.concatenate([wi[:, n_grp + 2 * HEADS:], wi[:, :n_grp]], axis=1).astype(bf16),
        'w_in_small': jnp.pad(wi[:, n_grp:n_grp + 2 * HEADS], ((0, 0), (0, LANES - 2 * HEADS))).astype(bf16),
        'lb': lbs[l][None, :],
        'hgrn_norm': hgrn_norm[l][None, :],
        'wq': jnp.tile(diff_qk_norm[l, 0], HW // DQK)[None, :],
        'wk': jnp.tile(diff_qk_norm[l, 1], HW // DQK)[None, :],
        'bd': jnp.where(r[:, None] == r[None, :], 1.0 / DQK, 0.0).astype(bf16),
        'lam': jnp.full((1, HD), lam, f32),
        'diff_subln': diff_subln[l][None, :],
        'gdn_conv': gdn_conv[l],
        'alog': lane_pad(gdn_a_log[l], HEADS),
        'dtb': lane_pad(gdn_dt_bias[l], HEADS),
        'gdn_norm': gdn_norm[l][None, :],
        'w_branch_a': w_branch_a[l].astype(bf16),
        'w_branch_b': w_branch_b[l].astype(bf16),
        'w_branch_c': w_branch_c[l].astype(bf16),
        'w_out': w_out[l].astype(bf16),
        'ln_ffn': ln_ffn[l][None, :],
        'w_up': w_up[l].astype(bf16),
        'ffn_conv': ffn_conv[l],
        'w_down': w_down[l].astype(bf16),
    }


def kernel(x_prompt, x_sample, state_hgrn, cache_k, cache_v, state_gdn, state_gdn_conv, state_ffn_conv, page_table, ln_mix, w_in, hgrn_lb, hgrn_norm, diff_qk_norm, diff_lambda, diff_subln, gdn_conv, gdn_a_log, gdn_dt_bias, gdn_norm, w_branch_a, w_branch_b, w_branch_c, w_out, ln_ffn, w_up, ffn_conv, w_down):
    depth = w_in.shape[0]
    Bp, Lp, D = x_prompt.shape
    Bs, Ls, _ = x_sample.shape
    past_len = page_table.shape[1] * cache_k.shape[2]

    lb_w = jax.nn.softmax(hgrn_lb.astype(f32), axis=0)
    lbs = jnp.cumsum(lb_w, axis=0) - lb_w[0]

    ck = cache_k.reshape(cache_k.shape[:2] + (-1, HD))
    cv = cache_v.reshape(cache_v.shape[:2] + (-1, HD))
    pt = page_table.reshape(-1).astype(jnp.int32)

    zeros_state = jnp.zeros((Bp, HEADS, HD, HD), f32)
    pad_conv = lambda b: jnp.pad(b, ((0, 0), (SUBLANES - (GD_CONV - 1), 0), (0, 0)))
    zeros_gconv = jnp.zeros((Bp, SUBLANES, 3 * HW), f32)
    zeros_fconv = jnp.zeros((Bp, FFN_CONV - 1, w_down.shape[1]), f32)

    yp = x_prompt.reshape(Bp * Lp, D)
    ys = x_sample.reshape(Bs * Ls, D)
    outs_p = [[] for _ in range(6)]
    outs_s = [[] for _ in range(6)]
    for l in range(depth):
        w = _layer_weights(l, lbs, ln_mix, w_in, hgrn_norm, diff_qk_norm, diff_lambda, diff_subln,
                           gdn_conv, gdn_a_log, gdn_dt_bias, gdn_norm, w_branch_a, w_branch_b,
                           w_branch_c, w_out, ln_ffn, w_up, ffn_conv, w_down)
        res = _layer(yp, Bp, Lp, 0, l, w, zeros_state, zeros_state, zeros_gconv, zeros_fconv, None)
        yp = res[0]
        for acc, r in zip(outs_p, res[1:]):
            acc.append(r)
        res = _layer(ys, Bs, Ls, past_len, l, w, state_hgrn[l], state_gdn[l],
                     pad_conv(state_gdn_conv[l]), state_ffn_conv[l], (pt, ck, cv))
        ys = res[0]
        for acc, r in zip(outs_s, res[1:]):
            acc.append(r)

    return ((yp.reshape(Bp, Lp, D), ys.reshape(Bs, Ls, D))
            + tuple(jnp.stack(a) for a in outs_p) + tuple(jnp.stack(a) for a in outs_s))
```
